```python
import math
import jax
import jax.numpy as jnp
from jax import lax
import numpy as np

D_MODEL = 2048
BATCH = 1
SEQ = 8192
DEPTH = 1

PLE_DIM = 256
Q_BLOCK = 128
ROPE_THETA = 10000.0
NORM_EPS = 1e-6

DIFF_HEADS = 8
DIFF_QK_DIM = 64
DIFF_V_DIM = 128
DIFF_WIDTH = DIFF_HEADS * DIFF_V_DIM
DIFF_QK_COLS = DIFF_HEADS * 2 * DIFF_QK_DIM

FOX_HEADS = 8
FOX_HEAD_DIM = 128
FOX_WIDTH = FOX_HEADS * FOX_HEAD_DIM

MIX_WIDTH = DIFF_WIDTH + FOX_WIDTH
IN_PROJ_COLS = 3 * DIFF_QK_COLS // 1 * 0 + 2 * DIFF_QK_COLS + DIFF_WIDTH + 3 * FOX_WIDTH + FOX_HEADS

N_EXPERTS = 64
TOP_K = 8
EXPERT_DIM = 512
SHARED_DIM = 512
ROUTED_SCALE = 2.5

kernel_name = "hybrid_diff_fox_moe_ple_layer"


def _rmsnorm(x, g):
    xf = x.astype(jnp.float32)
    y = xf * lax.rsqrt(jnp.mean(xf * xf, axis=-1, keepdims=True) + NORM_EPS)
    return (y * g.astype(jnp.float32)).astype(x.dtype)


def _rope_tables(seq, dim):
    pos = jnp.arange(seq, dtype=jnp.float32)
    inv_freq = ROPE_THETA ** (-jnp.arange(0, dim, 2, dtype=jnp.float32) / dim)
    ang = pos[:, None] * inv_freq[None, :]
    ang = jnp.concatenate([ang, ang], axis=-1)
    return jnp.cos(ang), jnp.sin(ang)


def _rope(x, cos, sin):
    xf = x.astype(jnp.float32)
    half = xf.shape[-1] // 2
    rot = jnp.concatenate([-xf[..., half:], xf[..., :half]], axis=-1)
    shape = (1, x.shape[1]) + (1,) * (x.ndim - 3) + (x.shape[-1],)
    return (xf * cos.reshape(shape) + rot * sin.reshape(shape)).astype(x.dtype)


def _diff_attention(q, k, v, lam, subln_g, lam_init):
    b, s = q.shape[:2]
    n_blocks = s // Q_BLOCK
    kpos = jnp.arange(s)
    scale = DIFF_QK_DIM ** -0.5

    def block(i):
        start = i * Q_BLOCK
        qb = lax.dynamic_slice_in_dim(q, start, Q_BLOCK, axis=1)
        sc = jnp.einsum('bqhcd,bkhcd->bhcqk', qb, k,
                        preferred_element_type=jnp.float32) * scale
        qpos = start + jnp.arange(Q_BLOCK)
        causal = kpos[None, :] <= qpos[:, None]
        a = jax.nn.softmax(jnp.where(causal, sc, -jnp.inf), axis=-1)
        a = a[:, :, 0] - lam * a[:, :, 1]
        return jnp.einsum('bhqk,bkhd->bqhd', a.astype(v.dtype), v)

    o = lax.map(block, jnp.arange(n_blocks))
    o = jnp.moveaxis(o, 0, 1).reshape(b, s, DIFF_HEADS, DIFF_V_DIM)
    o = _rmsnorm(o, subln_g) * (1.0 - lam_init)
    return o.reshape(b, s, DIFF_WIDTH)


def _forgetting_attention(q, k, v, log_f):
    b, s = q.shape[:2]
    n_blocks = s // Q_BLOCK
    kpos = jnp.arange(s)
    scale = FOX_HEAD_DIM ** -0.5
    c = jnp.moveaxis(jnp.cumsum(log_f, axis=1), 2, 1)

    def block(i):
        start = i * Q_BLOCK
        qb = lax.dynamic_slice_in_dim(q, start, Q_BLOCK, axis=1)
        cq = lax.dynamic_slice_in_dim(c, start, Q_BLOCK, axis=2)
        sc = jnp.einsum('bqhd,bkhd->bhqk', qb, k,
                        preferred_element_type=jnp.float32) * scale
        sc = sc + (cq[..., :, None] - c[..., None, :])
        qpos = start + jnp.arange(Q_BLOCK)
        causal = kpos[None, :] <= qpos[:, None]
        a = jax.nn.softmax(jnp.where(causal, sc, -jnp.inf), axis=-1)
        return jnp.einsum('bhqk,bkhd->bqhd', a.astype(v.dtype), v)

    o = lax.map(block, jnp.arange(n_blocks))
    return jnp.moveaxis(o, 0, 1).reshape(b, s, FOX_WIDTH)


def _swiglu(t, wg, wu, wd):
    return (jax.nn.silu(t @ wg) * (t @ wu)) @ wd


def _moe(x, w_router, router_bias, w_gate, w_up, w_down, ws_gate, ws_up, ws_down):
    b, s, d = x.shape
    t = x.reshape(b * s, d)
    scores = jax.nn.sigmoid(jnp.einsum('td,de->te', t, w_router,
                                       preferred_element_type=jnp.float32))
    _, idx = lax.top_k(scores + router_bias.astype(jnp.float32), TOP_K)
    g = jnp.take_along_axis(scores, idx, axis=-1)
    g = g / jnp.sum(g, axis=-1, keepdims=True) * ROUTED_SCALE
    comb = jnp.einsum('tk,tke->te', g,
                      jax.nn.one_hot(idx, N_EXPERTS, dtype=jnp.float32))

    def expert_step(acc, xs):
        wg, wu, wd, c = xs
        return acc + _swiglu(t, wg, wu, wd) * c[:, None].astype(acc.dtype), None

    y, _ = lax.scan(expert_step, jnp.zeros_like(t), (w_gate, w_up, w_down, comb.T))
    y = y + _swiglu(t, ws_gate, ws_up, ws_down)
    return y.reshape(b, s, d)


def setup_inputs(seed: int = 0) -> dict:
    key = jax.random.key(seed)
    ks = jax.random.split(key, 32)
    f32 = jnp.float32
    n = lambda k, shape, scale: jax.random.normal(k, shape, f32) * scale
    gain = lambda k, shape: 1.0 + 0.05 * jax.random.normal(k, shape, f32)
    L, D, E, F = DEPTH, D_MODEL, N_EXPERTS, EXPERT_DIM
    return {
        "x": jax.random.normal(ks[0], (BATCH, SEQ, D), f32),
        "p": jax.random.normal(ks[1], (DEPTH, BATCH, SEQ, PLE_DIM), f32),
        "attn_norm": gain(ks[2], (L, D)),
        "w_in": n(ks[3], (L, D, IN_PROJ_COLS), D ** -0.5),
        "b_forget": jax.random.uniform(ks[4], (L, FOX_HEADS), f32, 0.0, 5.0),
        "lambda_q1": n(ks[5], (L, DIFF_QK_DIM), 0.1),
        "lambda_k1": n(ks[6], (L, DIFF_QK_DIM), 0.1),
        "lambda_q2": n(ks[7], (L, DIFF_QK_DIM), 0.1),
        "lambda_k2": n(ks[8], (L, DIFF_QK_DIM), 0.1),
        "subln": gain(ks[9], (L, DIFF_V_DIM)),
        "w_o": n(ks[10], (L, MIX_WIDTH, D), MIX_WIDTH ** -0.5),
        "ffn_norm": gain(ks[11], (L, D)),
        "w_router": n(ks[12], (L, D, E), D ** -0.5),
        "router_bias": n(ks[13], (L, E), 0.01),
        "w_gate": n(ks[14], (L, E, D, F), D ** -0.5),
        "w_up": n(ks[15], (L, E, D, F), D ** -0.5),
        "w_down": n(ks[16], (L, E, F, D), F ** -0.5),
        "ws_gate": n(ks[17], (L, D, SHARED_DIM), D ** -0.5),
        "ws_up": n(ks[18], (L, D, SHARED_DIM), D ** -0.5),
        "ws_down": n(ks[19], (L, SHARED_DIM, D), SHARED_DIM ** -0.5),
        "w_ple_gate": n(ks[20], (L, D, D), D ** -0.5),
        "b_ple_gate": n(ks[21], (L, D), 0.02),
        "w_ple_proj": n(ks[22], (L, PLE_DIM, D), PLE_DIM ** -0.5),
        "ple_norm": gain(ks[23], (L, D)),
        "final_norm": gain(ks[24], (D,)),
    }


def reference(x, p, attn_norm, w_in, b_forget, lambda_q1, lambda_k1, lambda_q2,
              lambda_k2, subln, w_o, ffn_norm, w_router, router_bias, w_gate, w_up,
              w_down, ws_gate, ws_up, ws_down, w_ple_gate, b_ple_gate, w_ple_proj,
              ple_norm, final_norm):
    b, s, _ = x.shape
    cos, sin = _rope_tables(s, DIFF_QK_DIM)
    split_at = [DIFF_QK_COLS, 2 * DIFF_QK_COLS, 2 * DIFF_QK_COLS + DIFF_WIDTH,
                2 * DIFF_QK_COLS + DIFF_WIDTH + FOX_WIDTH,
                2 * DIFF_QK_COLS + DIFF_WIDTH + 2 * FOX_WIDTH,
                2 * DIFF_QK_COLS + DIFF_WIDTH + 3 * FOX_WIDTH]
    h = x
    for i in range(DEPTH):
        xn = _rmsnorm(h, attn_norm[i])
        proj = xn @ w_in[i]
        dq, dk, dv, fq, fk, fv, fg = jnp.split(proj, split_at, axis=-1)

        dq = _rope(dq.reshape(b, s, DIFF_HEADS, 2, DIFF_QK_DIM), cos, sin)
        dk = _rope(dk.reshape(b, s, DIFF_HEADS, 2, DIFF_QK_DIM), cos, sin)
        dv = dv.reshape(b, s, DIFF_HEADS, DIFF_V_DIM)
        lam_init = 0.8 - 0.6 * math.exp(-0.3 * i)
        lam = (jnp.exp(jnp.sum(lambda_q1[i].astype(jnp.float32) * lambda_k1[i].astype(jnp.float32)))
               - jnp.exp(jnp.sum(lambda_q2[i].astype(jnp.float32) * lambda_k2[i].astype(jnp.float32)))
               + lam_init)
        o_diff = _diff_attention(dq, dk, dv, lam, subln[i], lam_init)

        fq = fq.reshape(b, s, FOX_HEADS, FOX_HEAD_DIM)
        fk = fk.reshape(b, s, FOX_HEADS, FOX_HEAD_DIM)
        fv = fv.reshape(b, s, FOX_HEADS, FOX_HEAD_DIM)
        log_f = jax.nn.log_sigmoid((fg + b_forget[i]).astype(jnp.float32))
        o_fox = _forgetting_attention(fq, fk, fv, log_f)

        h = h + jnp.concatenate([o_diff, o_fox], axis=-1) @ w_o[i]

        h = h + _moe(_rmsnorm(h, ffn_norm[i]), w_router[i], router_bias[i],
                     w_gate[i], w_up[i], w_down[i], ws_gate[i], ws_up[i], ws_down[i])

        gate = jax.nn.sigmoid(h @ w_ple_gate[i] + b_ple_gate[i])
        h = h + gate * _rmsnorm(p[i] @ w_ple_proj[i], ple_norm[i])
    return _rmsnorm(h, final_norm)
```

```python
import functools
import math

import jax
import jax.numpy as jnp
from jax import lax
from jax.experimental import pallas as pl
from jax.experimental.pallas import tpu as pltpu

F32 = jnp.float32
BF16 = jnp.bfloat16

NORM_EPS = 1e-6
ROPE_THETA = 10000.0
HEADS = 8
HEAD_W = 128
DIFF_QK_DIM = 64
N_EXPERTS = 64
TOP_K = 8
ROUTED_SCALE = 2.5
LAM_INIT = 0.8 - 0.6 * math.exp(-0.3 * 0)
LANES = 128
NEG_BIG = -1e30
VMEM_LIMIT = 56 * 1024 * 1024


def _cparams(*sem):
    return pltpu.CompilerParams(dimension_semantics=sem, vmem_limit_bytes=VMEM_LIMIT)


def _split_bf16(a, terms):
    parts = []
    r = a
    for _ in range(terms):
        p = r.astype(BF16)
        parts.append(p)
        r = r - p.astype(F32)
    return parts


def _dot(a, b):
    return jnp.dot(a, b, preferred_element_type=F32)


def _dot_f32ish(a, b):
    a_hi, a_lo = _split_bf16(a, 2)
    b_hi, b_lo = _split_bf16(b, 2)
    return _dot(a_hi, b_hi) + _dot(a_lo, b_hi) + _dot(a_hi, b_lo)


def _rms(x, g):
    return x * lax.rsqrt(jnp.mean(x * x, axis=-1, keepdims=True) + NORM_EPS) * g


def _inproj_kernel(x_ref, g_ref, w_ref, wfg_ref, cos_ref, sin_ref, o_ref, fg_ref, xn_ref, *, tn):
    j = pl.program_id(1)

    @pl.when(j == 0)
    def _():
        xn = _rms(x_ref[...], g_ref[...])
        xn_ref[...] = xn.astype(BF16)
        fg_ref[...] = _dot_f32ish(xn, wfg_ref[...])

    acc = _dot(xn_ref[...], w_ref[...].astype(BF16))
    reps = tn // LANES
    n_rope = 2 * HEADS * HEAD_W // tn
    n_q = HEADS * HEAD_W // tn
    fq0 = 3 * n_q

    @pl.when(j < n_rope)
    def _():
        cos = jnp.concatenate([cos_ref[...]] * reps, axis=1)
        sin = jnp.concatenate([sin_ref[...]] * reps, axis=1)
        lane = lax.broadcasted_iota(jnp.int32, acc.shape, 1) % DIFF_QK_DIM
        rot = jnp.where(lane < DIFF_QK_DIM // 2,
                        pltpu.roll(acc, tn - DIFF_QK_DIM // 2, 1),
                        pltpu.roll(acc, DIFF_QK_DIM // 2, 1))
        y = acc * cos + rot * sin
        scale = jnp.where(j < n_q, DIFF_QK_DIM ** -0.5, 1.0).astype(F32)
        o_ref[...] = (y * scale).astype(BF16)

    @pl.when((j >= fq0) & (j < fq0 + n_q))
    def _():
        o_ref[...] = (acc * (HEAD_W ** -0.5)).astype(BF16)

    @pl.when((j >= n_rope) & ((j < fq0) | (j >= fq0 + n_q)))
    def _():
        o_ref[...] = acc.astype(BF16)


def _inproj(x, g, w_in, w_fg, cos, sin, *, tm, tn):
    t, d = x.shape
    ncol = 6 * HEADS * HEAD_W
    return pl.pallas_call(
        functools.partial(_inproj_kernel, tn=tn),
        grid=(t // tm, ncol // tn),
        in_specs=[
            pl.BlockSpec((tm, d), lambda i, j: (i, 0)),
            pl.BlockSpec((1, d), lambda i, j: (0, 0)),
            pl.BlockSpec((d, tn), lambda i, j: (0, j)),
            pl.BlockSpec((d, LANES), lambda i, j: (0, 0)),
            pl.BlockSpec((tm, LANES), lambda i, j: (i, 0)),
            pl.BlockSpec((tm, LANES), lambda i, j: (i, 0)),
        ],
        out_specs=[
            pl.BlockSpec((tm, tn), lambda i, j: (i, j)),
            pl.BlockSpec((tm, LANES), lambda i, j: (i, 0)),
        ],
        out_shape=[
            jax.ShapeDtypeStruct((t, ncol), BF16),
            jax.ShapeDtypeStruct((t, LANES), F32),
        ],
        scratch_shapes=[pltpu.VMEM((tm, d), BF16)],
        compiler_params=_cparams("arbitrary", "arbitrary"),
        name="inproj",
    )(x, g, w_in, w_fg, cos, sin)


def _forget_cumsum_kernel(fg_ref, b_ref, c_ref, carry_ref):
    i = pl.program_id(0)

    @pl.when(i == 0)
    def _():
        carry_ref[...] = jnp.zeros_like(carry_ref)

    z = fg_ref[...] + b_ref[...]
    logf = jnp.minimum(z, 0.0) - jnp.log(1.0 + jnp.exp(-jnp.abs(z)))
    tm = logf.shape[0]
    row = lax.broadcasted_iota(jnp.int32, (tm, tm), 0)
    col = lax.broadcasted_iota(jnp.int32, (tm, tm), 1)
    tri = (row >= col).astype(BF16)
    c = carry_ref[0:1, :]
    for part in _split_bf16(logf, 3):
        c = c + _dot(tri, part)
    c_ref[...] = c
    carry_ref[...] = jnp.broadcast_to(c[tm - 1:tm, :], carry_ref.shape)


def _forget_cumsum(fg, b, *, tm):
    t = fg.shape[0]
    return pl.pallas_call(
        _forget_cumsum_kernel,
        grid=(t // tm,),
        in_specs=[pl.BlockSpec((tm, LANES), lambda i: (i, 0)),
                  pl.BlockSpec((1, LANES), lambda i: (0, 0))],
        out_specs=pl.BlockSpec((tm, LANES), lambda i: (i, 0)),
        out_shape=jax.ShapeDtypeStruct((t, LANES), F32),
        scratch_shapes=[pltpu.VMEM((8, LANES), F32)],
        compiler_params=_cparams("arbitrary"),
        name="forget_cumsum",
    )(fg, b)


def _attn_kernel(*refs, ncomp, has_bias, tq, tk):
    if has_bias:
        q_ref, k_ref, v_ref, cq_ref, ck_ref, o_ref, qz_ref, m_ref, l_ref, acc_ref = refs
    else:
        q_ref, k_ref, v_ref, lam_ref, subln_ref, o_ref, qz_ref, m_ref, l_ref, acc_ref = refs
    h = pl.program_id(0)
    i = pl.program_id(1)
    rows = ncomp * tq

    q = q_ref[...]
    if ncomp == 2:
        lane = lax.broadcasted_iota(jnp.int32, q.shape, 1)
        zero = jnp.zeros_like(q)
        qz_ref[0:tq, :] = jnp.where(lane < DIFF_QK_DIM, q, zero)
        qz_ref[tq:rows, :] = jnp.where(lane >= DIFF_QK_DIM, q, zero)
    else:
        qz_ref[...] = q
    m_ref[...] = jnp.full_like(m_ref, NEG_BIG)
    l_ref[...] = jnp.zeros_like(l_ref)
    acc_ref[...] = jnp.zeros_like(acc_ref)

    if has_bias:
        lane = lax.broadcasted_iota(jnp.int32, cq_ref.shape, 1)
        cq = jnp.sum(jnp.where(lane == h, cq_ref[...], 0.0), axis=1, keepdims=True)

    def chunk(j, masked):
        start = pl.multiple_of(j * tk, tk)
        k = k_ref[pl.ds(start, tk), :]
        v = v_ref[pl.ds(start, tk), :]
        s = lax.dot_general(qz_ref[...], k, (((1,), (1,)), ((), ())),
                            preferred_element_type=F32)
        if has_bias:
            s = s + (cq - ck_ref[:, pl.ds(start, tk)])
        if masked:
            qpos = i * tq + lax.broadcasted_iota(jnp.int32, (rows, tk), 0) % tq
            kpos = start + lax.broadcasted_iota(jnp.int32, (rows, tk), 1)
            s = jnp.where(kpos <= qpos, s, NEG_BIG)
        m_prev = m_ref[...]
        m_new = jnp.maximum(m_prev, jnp.max(s, axis=1, keepdims=True))
        alpha = jnp.exp(m_prev - m_new)
        p = jnp.exp(s - m_new)
        l_ref[...] = alpha * l_ref[...] + jnp.sum(p, axis=1, keepdims=True)
        acc_ref[...] = alpha * acc_ref[...] + _dot(p.astype(BF16), v)
        m_ref[...] = m_new

    n_full = (i * tq) // tk

    def body(j, carry):
        chunk(j, False)
        return carry

    lax.fori_loop(0, n_full, body, 0)
    chunk(n_full, True)

    inv_l = 1.0 / l_ref[...]
    o = acc_ref[...] * inv_l
    if ncomp == 2:
        o = o[0:tq, :] - lam_ref[0, 0] * o[tq:rows, :]
        o = _rms(o, subln_ref[...]) * (1.0 - LAM_INIT)
    o_ref[...] = o.astype(o_ref.dtype)


def _attention(proj, q_blk, k_blk, v_blk, extra, *, ncomp, has_bias, tq, tk):
    t = proj.shape[0]
    rows = ncomp * tq
    in_specs = [
        pl.BlockSpec((tq, HEAD_W), lambda h, i: (i, q_blk + h)),
        pl.BlockSpec((t, HEAD_W), lambda h, i: (0, k_blk + h)),
        pl.BlockSpec((t, HEAD_W), lambda h, i: (0, v_blk + h)),
    ]
    if has_bias:
        c, c_t = extra
        in_specs += [pl.BlockSpec((tq, LANES), lambda h, i: (i, 0)),
                     pl.BlockSpec((None, 1, t), lambda h, i: (h, 0, 0))]
    else:
        lam, subln = extra
        in_specs += [pl.BlockSpec(memory_space=pltpu.SMEM),
                     pl.BlockSpec((1, HEAD_W), lambda h, i: (0, 0))]
    return pl.pallas_call(
        functools.partial(_attn_kernel, ncomp=ncomp, has_bias=has_bias, tq=tq, tk=tk),
        grid=(HEADS, t // tq),
        in_specs=in_specs,
        out_specs=pl.BlockSpec((tq, HEAD_W), lambda h, i: (i, h)),
        out_shape=jax.ShapeDtypeStruct((t, HEADS * HEAD_W), BF16),
        scratch_shapes=[pltpu.VMEM((rows, HEAD_W), BF16),
                        pltpu.VMEM((rows, 1), F32),
                        pltpu.VMEM((rows, 1), F32),
                        pltpu.VMEM((rows, HEAD_W), F32)],
        compiler_params=_cparams("arbitrary", "arbitrary"),
        name="fox_attention" if has_bias else "diff_attention",
    )(proj, proj, proj, *extra)


def _outproj_router_kernel(od_ref, of_ref, x_ref, wo_ref, g_ref, wr_ref, rb_ref,
                           h1_ref, hn_ref, comb_ref):
    half = od_ref.shape[1]
    h1 = x_ref[...] + _dot(od_ref[...], wo_ref[0:half, :]) + _dot(of_ref[...], wo_ref[half:, :])
    h1_ref[...] = h1
    hn = _rms(h1, g_ref[...])
    hn_ref[...] = hn.astype(BF16)
    scores = jax.nn.sigmoid(_dot_f32ish(hn, wr_ref[...]))
    lane = lax.broadcasted_iota(jnp.int32, scores.shape, 1)
    sel = jnp.where(lane < N_EXPERTS, scores + rb_ref[...], -jnp.inf)
    chosen = jnp.zeros(scores.shape, jnp.bool_)
    for _ in range(TOP_K):
        mx = jnp.max(sel, axis=1, keepdims=True)
        first = jnp.min(jnp.where(sel == mx, lane, LANES), axis=1, keepdims=True)
        hit = lane == first
        chosen = chosen | hit
        sel = jnp.where(hit, -jnp.inf, sel)
    g = jnp.where(chosen, scores, 0.0)
    comb_ref[...] = g / jnp.sum(g, axis=1, keepdims=True) * ROUTED_SCALE


def _outproj_router(o_diff, o_fox, x, w_o, ffn_norm, w_router, router_bias, *, tm):
    t, d = x.shape
    half = o_diff.shape[1]
    row = lambda i: (i, 0)
    const = lambda i: (0, 0)
    return pl.pallas_call(
        _outproj_router_kernel,
        grid=(t // tm,),
        in_specs=[
            pl.BlockSpec((tm, half), row),
            pl.BlockSpec((tm, half), row),
            pl.BlockSpec((tm, d), row),
            pl.BlockSpec((d, d), const),
            pl.BlockSpec((1, d), const),
            pl.BlockSpec((d, LANES), const),
            pl.BlockSpec((1, LANES), const),
        ],
        out_specs=[pl.BlockSpec((tm, d), row), pl.BlockSpec((tm, d), row),
                   pl.BlockSpec((tm, LANES), row)],
        out_shape=[jax.ShapeDtypeStruct((t, d), F32), jax.ShapeDtypeStruct((t, d), BF16),
                   jax.ShapeDtypeStruct((t, LANES), F32)],
        compiler_params=_cparams("arbitrary"),
        name="outproj_router",
    )(o_diff, o_fox, x, w_o, ffn_norm, w_router, router_bias)


def _moe_dense_kernel(x_ref, comb_ref, wg_ref, wu_ref, wd_ref, y_ref):
    e = pl.program_id(1)

    @pl.when(e == 0)
    def _():
        y_ref[...] = jnp.zeros_like(y_ref)

    x = x_ref[...]
    g = _dot(x, wg_ref[...])
    u = _dot(x, wu_ref[...])
    lane = lax.broadcasted_iota(jnp.int32, comb_ref.shape, 1)
    c = jnp.sum(jnp.where(lane == e, comb_ref[...], 0.0), axis=1, keepdims=True)
    hmid = (jax.nn.silu(g) * u * c).astype(BF16)
    y_ref[...] += _dot(hmid, wd_ref[...])


def _moe_dense(hn, comb, w_gate, w_up, w_down, *, tm):
    t, d = hn.shape
    n_e, _, f = w_gate.shape
    return pl.pallas_call(
        _moe_dense_kernel,
        grid=(t // tm, n_e),
        in_specs=[
            pl.BlockSpec((tm, d), lambda i, e: (i, 0)),
            pl.BlockSpec((tm, LANES), lambda i, e: (i, 0)),
            pl.BlockSpec((None, d, f), lambda i, e: (e, 0, 0)),
            pl.BlockSpec((None, d, f), lambda i, e: (e, 0, 0)),
            pl.BlockSpec((None, f, d), lambda i, e: (e, 0, 0)),
        ],
        out_specs=pl.BlockSpec((tm, d), lambda i, e: (i, 0)),
        out_shape=jax.ShapeDtypeStruct((t, d), F32),
        compiler_params=_cparams("arbitrary", "arbitrary"),
        name="moe_dense",
    )(hn, comb, w_gate, w_up, w_down)


def _final_kernel(h1_ref, y_ref, hn_ref, p_ref, wsg_ref, wsu_ref, wsd_ref, wpg_ref, bpg_ref,
                  wpp_ref, gple_ref, gfin_ref, o_ref):
    hn = hn_ref[...]
    mid = (jax.nn.silu(_dot(hn, wsg_ref[...])) * _dot(hn, wsu_ref[...])).astype(BF16)
    h2 = h1_ref[...] + y_ref[...] + _dot(mid, wsd_ref[...])
    gate = jax.nn.sigmoid(_dot(h2.astype(BF16), wpg_ref[...]) + bpg_ref[...])
    emb = _rms(_dot(p_ref[...].astype(BF16), wpp_ref[...]), gple_ref[...])
    o_ref[...] = _rms(h2 + gate * emb, gfin_ref[...])


def _final(h1, y, hn, p, ws_gate, ws_up, ws_down, w_ple_gate, b_ple_gate, w_ple_proj,
           ple_norm, final_norm, *, tm):
    t, d = h1.shape
    row = lambda i: (i, 0)
    const = lambda i: (0, 0)
    full = lambda a: pl.BlockSpec(a.shape, const)
    return pl.pallas_call(
        _final_kernel,
        grid=(t // tm,),
        in_specs=[pl.BlockSpec((tm, d), row), pl.BlockSpec((tm, d), row), pl.BlockSpec((tm, d), row),
                  pl.BlockSpec((tm, p.shape[1]), row),
                  full(ws_gate), full(ws_up), full(ws_down), full(w_ple_gate), full(b_ple_gate),
                  full(w_ple_proj), full(ple_norm), full(final_norm)],
        out_specs=pl.BlockSpec((tm, d), row),
        out_shape=jax.ShapeDtypeStruct((t, d), F32),
        compiler_params=_cparams("arbitrary"),
        name="final",
    )(h1, y, hn, p, ws_gate, ws_up, ws_down, w_ple_gate, b_ple_gate, w_ple_proj, ple_norm, final_norm)


def _rope_tables(t):
    pos = jnp.arange(t, dtype=F32)
    inv_freq = ROPE_THETA ** (-jnp.arange(0, DIFF_QK_DIM, 2, dtype=F32) / DIFF_QK_DIM)
    ang = pos[:, None] * inv_freq[None, :]
    ang = jnp.concatenate([ang, ang, ang, ang], axis=-1)
    half_sign = jnp.where(jnp.arange(LANES) % DIFF_QK_DIM < DIFF_QK_DIM // 2, -1.0, 1.0).astype(F32)
    return jnp.cos(ang), jnp.sin(ang) * half_sign


def _pad_lanes(a):
    return jnp.pad(a, ((0, 0), (0, LANES - a.shape[1])))


def kernel(x, p, attn_norm, w_in, b_forget, lambda_q1, lambda_k1, lambda_q2, lambda_k2, subln, w_o,
           ffn_norm, w_router, router_bias, w_gate, w_up, w_down, ws_gate, ws_up, ws_down,
           w_ple_gate, b_ple_gate, w_ple_proj, ple_norm, final_norm):
    b, t, d = x.shape
    assert b == 1 and w_in.shape[0] == 1
    x2 = x[0]
    slab = HEADS * HEAD_W
    ncol = 6 * slab

    cos, sin = _rope_tables(t)
    w_fg = _pad_lanes(w_in[0][:, ncol:])
    tm_in = min(t, 1024)
    proj, fg = _inproj(x2, attn_norm, w_in[0], w_fg, cos, sin, tm=tm_in, tn=512)

    c = _forget_cumsum(fg, _pad_lanes(b_forget), tm=min(t, 1024))
    c_t = c[:, :HEADS].T.reshape(HEADS, 1, t)

    lam = (jnp.exp(jnp.sum(lambda_q1[0] * lambda_k1[0])) - jnp.exp(jnp.sum(lambda_q2[0] * lambda_k2[0]))
           + LAM_INIT).reshape(1, 1).astype(F32)
    blk = slab // HEAD_W
    o_diff = _attention(proj, 0, blk, 2 * blk, (lam, subln), ncomp=2, has_bias=False,
                        tq=min(t, 256), tk=min(t, 512))
    o_fox = _attention(proj, 3 * blk, 4 * blk, 5 * blk, (c, c_t), ncomp=1, has_bias=True,
                       tq=min(t, 512), tk=min(t, 512))

    h1, hn, comb = _outproj_router(o_diff, o_fox, x2, w_o[0].astype(BF16), ffn_norm,
                                   _pad_lanes(w_router[0]), _pad_lanes(router_bias), tm=min(t, 512))

    y = _moe_dense(hn, comb, w_gate[0].astype(BF16), w_up[0].astype(BF16), w_down[0].astype(BF16),
                   tm=min(t, 1024))

    out = _final(h1, y, hn, p[0, 0], ws_gate[0].astype(BF16), ws_up[0].astype(BF16),
                 ws_down[0].astype(BF16), w_ple_gate[0].astype(BF16), b_ple_gate,
                 w_ple_proj[0].astype(BF16), ple_norm, final_norm.reshape(1, d), tm=min(t, 256))
    return out.reshape(b, t, d)
```

```python
import functools
import math

import jax
import jax.numpy as jnp
import numpy as np
from jax import lax
from jax.experimental import pallas as pl
from jax.experimental.pallas import tpu as pltpu

F32 = jnp.float32
BF16 = jnp.bfloat16

NORM_EPS = 1e-6
ROPE_THETA = 10000.0
HEADS = 8
HEAD_W = 128
DIFF_QK_DIM = 64
N_EXPERTS = 64
TOP_K = 8
ROUTED_SCALE = 2.5
LAM_INIT = 0.8 - 0.6 * math.exp(-0.3 * 0)
LOG2E = 1.4426950408889634
LANES = 128
NEG_BIG = -1e30
VMEM_LIMIT = 56 * 1024 * 1024
TC = 512
AUG_ROWS = 32
N_PARTS = 3


def _cparams(*sem):
    return pltpu.CompilerParams(dimension_semantics=sem, vmem_limit_bytes=VMEM_LIMIT)


def _split_bf16(a, terms):
    parts = []
    r = a
    for _ in range(terms):
        p = r.astype(BF16)
        parts.append(p)
        r = r - p.astype(F32)
    return parts


def _dot(a, b):
    return jnp.dot(a, b, preferred_element_type=F32)


def _dot_f32ish(a, b):
    a_hi, a_lo = _split_bf16(a, 2)
    b_hi, b_lo = _split_bf16(b, 2)
    return _dot(a_hi, b_hi) + _dot(a_lo, b_hi) + _dot(a_hi, b_lo)


def _rms(x, g):
    return x * lax.rsqrt(jnp.mean(x * x, axis=-1, keepdims=True) + NORM_EPS) * g


_SLAB_TRANSPOSED = (True, False, True, True, False, True)
_SLAB_ROPE = (True, True, False, False, False, False)
_SLAB_SCALE = (DIFF_QK_DIM ** -0.5 * LOG2E, 1.0, 1.0, HEAD_W ** -0.5 * LOG2E, 1.0, 1.0)


def _inproj_tables(tn):
    per = HEADS * HEAD_W // tn
    nmap, tmap, mode = [], [], []
    n_seen = t_seen = 0
    n_total = per * _SLAB_TRANSPOSED.count(False)
    t_total = per * _SLAB_TRANSPOSED.count(True)
    for s in range(len(_SLAB_TRANSPOSED)):
        for _ in range(per):
            nmap.append(min(n_seen, n_total - 1))
            tmap.append(min(t_seen, t_total - 1))
            mode.append(s)
            if _SLAB_TRANSPOSED[s]:
                t_seen += 1
            else:
                n_seen += 1
    return (np.asarray(nmap, np.int32), np.asarray(tmap, np.int32), np.asarray(mode, np.int32))


def _inproj_kernel(nmap_ref, tmap_ref, mode_ref, x_ref, g_ref, w_ref, wfg_ref, cos_ref, sin_ref,
                   kn_ref, pt_ref, fg_ref, xn_ref, *, tn):
    j = pl.program_id(1)

    @pl.when(j == 0)
    def _():
        xn = _rms(x_ref[...], g_ref[...])
        xn_ref[...] = xn.astype(BF16)
        fg_ref[...] = _dot_f32ish(xn, wfg_ref[...])

    acc = _dot(xn_ref[...], w_ref[...].astype(BF16))
    reps = tn // LANES
    slab = mode_ref[j]

    def rope(a):
        cos = jnp.concatenate([cos_ref[...]] * reps, axis=1)
        sin = jnp.concatenate([sin_ref[...]] * reps, axis=1)
        lane = lax.broadcasted_iota(jnp.int32, a.shape, 1) % DIFF_QK_DIM
        rot = jnp.where(lane < DIFF_QK_DIM // 2,
                        pltpu.roll(a, tn - DIFF_QK_DIM // 2, 1),
                        pltpu.roll(a, DIFF_QK_DIM // 2, 1))
        return a * cos + rot * sin

    def store_t(a):
        at = a.T.astype(BF16)
        for c in range(pt_ref.shape[0]):
            pt_ref[c] = at[:, c * TC:(c + 1) * TC]

    for s in range(len(_SLAB_TRANSPOSED)):
        @pl.when(slab == s)
        def _(s=s):
            y = rope(acc) if _SLAB_ROPE[s] else acc
            if _SLAB_SCALE[s] != 1.0:
                y = y * _SLAB_SCALE[s]
            if _SLAB_TRANSPOSED[s]:
                store_t(y)
            else:
                kn_ref[...] = y.astype(BF16)


def _inproj(x, g, w_in, w_fg, cos, sin, *, tm, tn):
    t, d = x.shape
    slab = HEADS * HEAD_W
    nmap, tmap, mode = _inproj_tables(tn)
    n_t = _SLAB_TRANSPOSED.count(True)
    n_n = _SLAB_TRANSPOSED.count(False)
    grid_spec = pltpu.PrefetchScalarGridSpec(
        num_scalar_prefetch=3,
        grid=(t // tm, len(mode)),
        in_specs=[
            pl.BlockSpec((tm, d), lambda i, j, *_: (i, 0)),
            pl.BlockSpec((1, d), lambda i, j, *_: (0, 0)),
            pl.BlockSpec((d, tn), lambda i, j, *_: (0, j)),
            pl.BlockSpec((d, LANES), lambda i, j, *_: (0, 0)),
            pl.BlockSpec((tm, LANES), lambda i, j, *_: (i, 0)),
            pl.BlockSpec((tm, LANES), lambda i, j, *_: (i, 0)),
        ],
        out_specs=[
            pl.BlockSpec((tm, tn), lambda i, j, nmap, tmap, mode: (i, nmap[j])),
            pl.BlockSpec((tm // TC, tn, TC), lambda i, j, nmap, tmap, mode: (i, tmap[j], 0)),
            pl.BlockSpec((tm, LANES), lambda i, j, *_: (i, 0)),
        ],
        scratch_shapes=[pltpu.VMEM((tm, d), BF16)],
    )
    return pl.pallas_call(
        functools.partial(_inproj_kernel, tn=tn),
        grid_spec=grid_spec,
        out_shape=[
            jax.ShapeDtypeStruct((t, n_n * slab), BF16),
            jax.ShapeDtypeStruct((t // TC, n_t * slab, TC), BF16),
            jax.ShapeDtypeStruct((t, LANES), F32),
        ],
        compiler_params=_cparams("arbitrary", "arbitrary"),
        name="inproj",
    )(nmap, tmap, mode, x, g, w_in, w_fg, cos, sin)


def _forget_cumsum_kernel(fg_ref, b_ref, ka_ref, qat_ref, carry_ref):
    i = pl.program_id(0)

    @pl.when(i == 0)
    def _():
        carry_ref[...] = jnp.zeros_like(carry_ref)

    z = fg_ref[...] + b_ref[...]
    logf = jnp.minimum(z, 0.0) - jnp.log(1.0 + jnp.exp(-jnp.abs(z)))
    tm = logf.shape[0]
    row = lax.broadcasted_iota(jnp.int32, (tm, tm), 0)
    col = lax.broadcasted_iota(jnp.int32, (tm, tm), 1)
    tri = (row >= col).astype(BF16)
    c = carry_ref[0:1, :]
    for part in _split_bf16(logf, N_PARTS):
        c = c + _dot(tri, part)
    carry_ref[...] = jnp.broadcast_to(c[tm - 1:tm, :], carry_ref.shape)

    parts = [p.astype(F32) for p in _split_bf16(c * LOG2E, N_PARTS)]
    lane = lax.broadcasted_iota(jnp.int32, (tm, LANES), 1)
    ka = jnp.where((lane >= N_PARTS * HEADS) & (lane < N_PARTS * HEADS + N_PARTS), 1.0, 0.0)
    for p in range(N_PARTS):
        shifted = parts[p] if p == 0 else pltpu.roll(parts[p], p * HEADS, 1)
        ka = jnp.where((lane >= p * HEADS) & (lane < (p + 1) * HEADS), -shifted, ka)
    ka_ref[...] = ka.astype(BF16)

    parts_t = [p.T for p in parts]
    r = lax.broadcasted_iota(jnp.int32, (AUG_ROWS, tm), 0)
    for h in range(HEADS):
        blk = jnp.where((r < N_PARTS * HEADS) & (r % HEADS == h), 1.0, 0.0)
        for p in range(N_PARTS):
            blk = jnp.where(r == N_PARTS * HEADS + p, parts_t[p][h:h + 1, :], blk)
        blk = blk.astype(BF16)
        for ci in range(qat_ref.shape[0]):
            qat_ref[ci, h * AUG_ROWS:(h + 1) * AUG_ROWS, :] = blk[:, ci * TC:(ci + 1) * TC]


def _forget_cumsum(fg, b, *, tm):
    t = fg.shape[0]
    return pl.pallas_call(
        _forget_cumsum_kernel,
        grid=(t // tm,),
        in_specs=[pl.BlockSpec((tm, LANES), lambda i: (i, 0)),
                  pl.BlockSpec((1, LANES), lambda i: (0, 0))],
        out_specs=[pl.BlockSpec((tm, LANES), lambda i: (i, 0)),
                   pl.BlockSpec((tm // TC, HEADS * AUG_ROWS, TC), lambda i: (i, 0, 0))],
        out_shape=[jax.ShapeDtypeStruct((t, LANES), BF16),
                   jax.ShapeDtypeStruct((t // TC, HEADS * AUG_ROWS, TC), BF16)],
        scratch_shapes=[pltpu.VMEM((8, LANES), F32)],
        compiler_params=_cparams("arbitrary"),
        name="forget_cumsum",
    )(fg, b)


def _attn_kernel(*refs, ncomp, has_bias, tq):
    if has_bias:
        (qt_ref, k_ref, vt_ref, ka_ref, qat_ref, o_ref,
         qz_ref, m_ref, l_ref, acc_ref, sa_ref, sb_ref, kaug_ref) = refs
    else:
        (qt_ref, k_ref, vt_ref, lam_ref, subln_ref, o_ref,
         qz_ref, m_ref, l_ref, acc_ref, sa_ref, sb_ref) = refs
    i = pl.program_id(1)
    cols = ncomp * tq
    tk = TC

    qt = qt_ref[...]
    if ncomp == 2:
        row = lax.broadcasted_iota(jnp.int32, qt.shape, 0)
        zero = jnp.zeros_like(qt)
        qz_ref[:, 0:tq] = jnp.where(row < DIFF_QK_DIM, qt, zero)
        qz_ref[:, tq:cols] = jnp.where(row >= DIFF_QK_DIM, qt, zero)
        ksrc = k_ref
    else:
        qz_ref[0:HEAD_W, :] = qt
        qz_ref[HEAD_W:HEAD_W + AUG_ROWS, :] = qat_ref[...]
        qz_ref[HEAD_W + AUG_ROWS:, :] = jnp.zeros((qz_ref.shape[0] - HEAD_W - AUG_ROWS, cols), BF16)

        @pl.when(i == 0)
        def _():
            kaug_ref[:, 0:HEAD_W] = k_ref[...]
            kaug_ref[:, HEAD_W:] = ka_ref[...]
        ksrc = kaug_ref

    m_ref[...] = jnp.full_like(m_ref, NEG_BIG)
    l_ref[...] = jnp.zeros_like(l_ref)
    acc_ref[...] = jnp.zeros_like(acc_ref)

    def scores(j, s_ref, masked):
        start = pl.multiple_of(j * tk, tk)
        s = _dot(ksrc[pl.ds(start, tk), :], qz_ref[...])
        if masked:
            kpos = start + lax.broadcasted_iota(jnp.int32, (tk, cols), 0)
            qpos = i * tq + lax.broadcasted_iota(jnp.int32, (tk, cols), 1) % tq
            s = jnp.where(kpos <= qpos, s, NEG_BIG)
        s_ref[...] = s

    def absorb(j, s_ref):
        s = s_ref[...]
        m_prev = m_ref[...]
        m_new = jnp.maximum(m_prev, jnp.max(s, axis=0, keepdims=True))
        alpha = jnp.exp2(m_prev - m_new)
        p = jnp.exp2(s - m_new)
        l_ref[...] = alpha * l_ref[...] + jnp.sum(p, axis=0, keepdims=True)
        acc_ref[...] = alpha * acc_ref[...] + _dot(vt_ref[j], p.astype(BF16))
        m_ref[...] = m_new

    n_full = (i * tq) // tk
    n_pairs = jnp.maximum(n_full - 1, 0) // 2
    scores(0, sa_ref, True)

    def pair(g, carry):
        c = 2 * g
        scores(c + 1, sb_ref, False)
        absorb(c, sa_ref)
        scores(c + 2, sa_ref, False)
        absorb(c + 1, sb_ref)
        return carry

    lax.fori_loop(0, n_pairs, pair, 0)
    c0 = 2 * n_pairs
    left = n_full - c0 + 1

    @pl.when(left == 1)
    def _():
        absorb(c0, sa_ref)

    @pl.when(left == 2)
    def _():
        scores(c0 + 1, sb_ref, True)
        absorb(c0, sa_ref)
        absorb(c0 + 1, sb_ref)

    @pl.when(left == 3)
    def _():
        scores(c0 + 1, sb_ref, False)
        absorb(c0, sa_ref)
        scores(c0 + 2, sa_ref, True)
        absorb(c0 + 1, sb_ref)
        absorb(c0 + 2, sa_ref)

    o = acc_ref[...] * (1.0 / l_ref[...])
    if ncomp == 2:
        o = o[:, 0:tq] - lam_ref[0, 0] * o[:, tq:cols]
        ms = jnp.mean(o * o, axis=0, keepdims=True)
        o = o * lax.rsqrt(ms + NORM_EPS) * subln_ref[...] * (1.0 - LAM_INIT)
    o_ref[...] = o.T.astype(o_ref.dtype)


def _attention(kn, pt, q_blk, k_blk, v_blk, extra, *, ncomp, has_bias, tq):
    t = kn.shape[0]
    cols = ncomp * tq
    per = TC // tq
    in_specs = [
        pl.BlockSpec((None, HEAD_W, tq), lambda h, i: (i // per, q_blk + h, i % per)),
        pl.BlockSpec((t, HEAD_W), lambda h, i: (0, k_blk + h)),
        pl.BlockSpec((t // TC, HEAD_W, TC), lambda h, i: (0, v_blk + h, 0)),
    ]
    scratch = [pltpu.VMEM(((2 if has_bias else 1) * HEAD_W, cols), BF16),
               pltpu.VMEM((1, cols), F32),
               pltpu.VMEM((1, cols), F32),
               pltpu.VMEM((HEAD_W, cols), F32),
               pltpu.VMEM((TC, cols), F32),
               pltpu.VMEM((TC, cols), F32)]
    if has_bias:
        assert tq == TC
        in_specs += [pl.BlockSpec((t, LANES), lambda h, i: (0, 0)),
                     pl.BlockSpec((None, AUG_ROWS, tq), lambda h, i: (i, h, 0))]
        scratch += [pltpu.VMEM((t, 2 * HEAD_W), BF16)]
    else:
        in_specs += [pl.BlockSpec(memory_space=pltpu.SMEM),
                     pl.BlockSpec((HEAD_W, 1), lambda h, i: (0, 0))]
    return pl.pallas_call(
        functools.partial(_attn_kernel, ncomp=ncomp, has_bias=has_bias, tq=tq),
        grid=(HEADS, t // tq),
        in_specs=in_specs,
        out_specs=pl.BlockSpec((tq, HEAD_W), lambda h, i: (i, h)),
        out_shape=jax.ShapeDtypeStruct((t, HEADS * HEAD_W), BF16),
        scratch_shapes=scratch,
        compiler_params=_cparams("arbitrary", "arbitrary"),
        name="fox_attention" if has_bias else "diff_attention",
    )(pt, kn, pt, *extra)


def _outproj_router_kernel(od_ref, of_ref, x_ref, wo_ref, g_ref, wr_ref, rb_ref,
                           h1_ref, hn_ref, comb_ref):
    half = od_ref.shape[1]
    h1 = x_ref[...] + _dot(od_ref[...], wo_ref[0:half, :]) + _dot(of_ref[...], wo_ref[half:, :])
    h1_ref[...] = h1
    hn = _rms(h1, g_ref[...])
    hn_ref[...] = hn.astype(BF16)
    scores = jax.nn.sigmoid(_dot_f32ish(hn, wr_ref[...]))
    lane = lax.broadcasted_iota(jnp.int32, scores.shape, 1)
    sel = jnp.where(lane < N_EXPERTS, scores + rb_ref[...], -jnp.inf)
    chosen = jnp.zeros(scores.shape, jnp.bool_)
    for _ in range(TOP_K):
        mx = jnp.max(sel, axis=1, keepdims=True)
        first = jnp.min(jnp.where(sel == mx, lane, LANES), axis=1, keepdims=True)
        hit = lane == first
        chosen = chosen | hit
        sel = jnp.where(hit, -jnp.inf, sel)
    g = jnp.where(chosen, scores, 0.0)
    comb_ref[...] = g / jnp.sum(g, axis=1, keepdims=True) * ROUTED_SCALE


def _outproj_router(o_diff, o_fox, x, w_o, ffn_norm, w_router, router_bias, *, tm):
    t, d = x.shape
    half = o_diff.shape[1]
    row = lambda i: (i, 0)
    const = lambda i: (0, 0)
    return pl.pallas_call(
        _outproj_router_kernel,
        grid=(t // tm,),
        in_specs=[
            pl.BlockSpec((tm, half), row),
            pl.BlockSpec((tm, half), row),
            pl.BlockSpec((tm, d), row),
            pl.BlockSpec((d, d), const),
            pl.BlockSpec((1, d), const),
            pl.BlockSpec((d, LANES), const),
            pl.BlockSpec((1, LANES), const),
        ],
        out_specs=[pl.BlockSpec((tm, d), row), pl.BlockSpec((tm, d), row),
                   pl.BlockSpec((tm, LANES), row)],
        out_shape=[jax.ShapeDtypeStruct((t, d), F32), jax.ShapeDtypeStruct((t, d), BF16),
                   jax.ShapeDtypeStruct((t, LANES), F32)],
        compiler_params=_cparams("arbitrary"),
        name="outproj_router",
    )(o_diff, o_fox, x, w_o, ffn_norm, w_router, router_bias)


def _moe_dense_kernel(x_ref, comb_ref, wg_ref, wu_ref, wd_ref, y_ref):
    e = pl.program_id(1)

    @pl.when(e == 0)
    def _():
        y_ref[...] = jnp.zeros_like(y_ref)

    x = x_ref[...]
    g = _dot(x, wg_ref[...])
    u = _dot(x, wu_ref[...])
    lane = lax.broadcasted_iota(jnp.int32, comb_ref.shape, 1)
    c = jnp.sum(jnp.where(lane == e, comb_ref[...], 0.0), axis=1, keepdims=True)
    hmid = (jax.nn.silu(g) * u * c).astype(BF16)
    y_ref[...] += _dot(hmid, wd_ref[...])


def _moe_dense(hn, comb, w_gate, w_up, w_down, *, tm):
    t, d = hn.shape
    n_e, _, f = w_gate.shape
    return pl.pallas_call(
        _moe_dense_kernel,
        grid=(t // tm, n_e),
        in_specs=[
            pl.BlockSpec((tm, d), lambda i, e: (i, 0)),
            pl.BlockSpec((tm, LANES), lambda i, e: (i, 0)),
            pl.BlockSpec((None, d, f), lambda i, e: (e, 0, 0)),
            pl.BlockSpec((None, d, f), lambda i, e: (e, 0, 0)),
            pl.BlockSpec((None, f, d), lambda i, e: (e, 0, 0)),
        ],
        out_specs=pl.BlockSpec((tm, d), lambda i, e: (i, 0)),
        out_shape=jax.ShapeDtypeStruct((t, d), F32),
        compiler_params=_cparams("arbitrary", "arbitrary"),
        name="moe_dense",
    )(hn, comb, w_gate, w_up, w_down)


def _final_kernel(h1_ref, y_ref, hn_ref, p_ref, wsg_ref, wsu_ref, wsd_ref, wpg_ref, bpg_ref,
                  wpp_ref, gple_ref, gfin_ref, o_ref):
    hn = hn_ref[...]
    mid = (jax.nn.silu(_dot(hn, wsg_ref[...])) * _dot(hn, wsu_ref[...])).astype(BF16)
    h2 = h1_ref[...] + y_ref[...] + _dot(mid, wsd_ref[...])
    gate = jax.nn.sigmoid(_dot(h2.astype(BF16), wpg_ref[...]) + bpg_ref[...])
    emb = _rms(_dot(p_ref[...].astype(BF16), wpp_ref[...]), gple_ref[...])
    o_ref[...] = _rms(h2 + gate * emb, gfin_ref[...])


def _final(h1, y, hn, p, ws_gate, ws_up, ws_down, w_ple_gate, b_ple_gate, w_ple_proj,
           ple_norm, final_norm, *, tm):
    t, d = h1.shape
    row = lambda i: (i, 0)
    const = lambda i: (0, 0)
    full = lambda a: pl.BlockSpec(a.shape, const)
    return pl.pallas_call(
        _final_kernel,
        grid=(t // tm,),
        in_specs=[pl.BlockSpec((tm, d), row), pl.BlockSpec((tm, d), row), pl.BlockSpec((tm, d), row),
                  pl.BlockSpec((tm, p.shape[1]), row),
                  full(ws_gate), full(ws_up), full(ws_down), full(w_ple_gate), full(b_ple_gate),
                  full(w_ple_proj), full(ple_norm), full(final_norm)],
        out_specs=pl.BlockSpec((tm, d), row),
        out_shape=jax.ShapeDtypeStruct((t, d), F32),
        compiler_params=_cparams("arbitrary"),
        name="final",
    )(h1, y, hn, p, ws_gate, ws_up, ws_down, w_ple_gate, b_ple_gate, w_ple_proj, ple_norm, final_norm)


def _rope_tables(t):
    pos = jnp.arange(t, dtype=F32)
    inv_freq = ROPE_THETA ** (-jnp.arange(0, DIFF_QK_DIM, 2, dtype=F32) / DIFF_QK_DIM)
    ang = pos[:, None] * inv_freq[None, :]
    ang = jnp.concatenate([ang, ang, ang, ang], axis=-1)
    half_sign = jnp.where(jnp.arange(LANES) % DIFF_QK_DIM < DIFF_QK_DIM // 2, -1.0, 1.0).astype(F32)
    return jnp.cos(ang), jnp.sin(ang) * half_sign


def _pad_lanes(a):
    return jnp.pad(a, ((0, 0), (0, LANES - a.shape[1])))


def kernel(x, p, attn_norm, w_in, b_forget, lambda_q1, lambda_k1, lambda_q2, lambda_k2, subln, w_o,
           ffn_norm, w_router, router_bias, w_gate, w_up, w_down, ws_gate, ws_up, ws_down,
           w_ple_gate, b_ple_gate, w_ple_proj, ple_norm, final_norm):
    b, t, d = x.shape
    assert b == 1 and w_in.shape[0] == 1 and t % TC == 0
    x2 = x[0]
    slab = HEADS * HEAD_W
    ncol = 6 * slab

    cos, sin = _rope_tables(t)
    w_fg = _pad_lanes(w_in[0][:, ncol:])
    kn, pt, fg = _inproj(x2, attn_norm, w_in[0], w_fg, cos, sin, tm=min(t, 1024), tn=512)

    ka, qat = _forget_cumsum(fg, _pad_lanes(b_forget), tm=min(t, 1024))

    lam = (jnp.exp(jnp.sum(lambda_q1[0] * lambda_k1[0])) - jnp.exp(jnp.sum(lambda_q2[0] * lambda_k2[0]))
           + LAM_INIT).reshape(1, 1).astype(F32)
    o_diff = _attention(kn, pt, 0, 0, HEADS, (lam, subln.reshape(HEAD_W, 1)),
                        ncomp=2, has_bias=False, tq=TC // 2)
    o_fox = _attention(kn, pt, 2 * HEADS, HEADS, 3 * HEADS, (ka, qat),
                       ncomp=1, has_bias=True, tq=TC)

    h1, hn, comb = _outproj_router(o_diff, o_fox, x2, w_o[0].astype(BF16), ffn_norm,
                                   _pad_lanes(w_router[0]), _pad_lanes(router_bias), tm=min(t, 512))

    y = _moe_dense(hn, comb, w_gate[0].astype(BF16), w_up[0].astype(BF16), w_down[0].astype(BF16),
                   tm=min(t, 1024))

    out = _final(h1, y, hn, p[0, 0], ws_gate[0].astype(BF16), ws_up[0].astype(BF16),
                 ws_down[0].astype(BF16), w_ple_gate[0].astype(BF16), b_ple_gate,
                 w_ple_proj[0].astype(BF16), ple_norm, final_norm.reshape(1, d), tm=min(t, 256))
    return out.reshape(b, t, d)
```

```python
import functools
import math

import jax
import jax.numpy as jnp
import numpy as np
from jax import lax
from jax.experimental import pallas as pl
from jax.experimental.pallas import tpu as pltpu

F32 = jnp.float32
BF16 = jnp.bfloat16

NORM_EPS = 1e-6
ROPE_THETA = 10000.0
HEADS = 8
HEAD_W = 128
DIFF_QK_DIM = 64
N_EXPERTS = 64
TOP_K = 8
ROUTED_SCALE = 2.5
LAM_INIT = 0.8 - 0.6 * math.exp(-0.3 * 0)
LOG2E = 1.4426950408889634
LANES = 128
NEG_BIG = -1e30
VMEM_LIMIT = 56 * 1024 * 1024
TC = 512
AUG_ROWS = 32
N_PARTS = 3
ROW_TILE = 256


def _cparams(*sem):
    return pltpu.CompilerParams(dimension_semantics=sem, vmem_limit_bytes=VMEM_LIMIT)


def _split_bf16(a, terms):
    parts = []
    r = a
    for _ in range(terms):
        p = r.astype(BF16)
        parts.append(p)
        r = r - p.astype(F32)
    return parts


def _dot(a, b):
    return jnp.dot(a, b, preferred_element_type=F32)


def _dot_f32ish(a, b):
    a_hi, a_lo = _split_bf16(a, 2)
    b_hi, b_lo = _split_bf16(b, 2)
    return _dot(a_hi, b_hi) + _dot(a_lo, b_hi) + _dot(a_hi, b_lo)


def _rms(x, g):
    return x * lax.rsqrt(jnp.mean(x * x, axis=-1, keepdims=True) + NORM_EPS) * g


_SLAB_TRANSPOSED = (True, False, True, True, False, True)
_SLAB_ROPE = (True, True, False, False, False, False)
_SLAB_SCALE = (DIFF_QK_DIM ** -0.5 * LOG2E, 1.0, 1.0, HEAD_W ** -0.5 * LOG2E, 1.0, 1.0)


def _inproj_tables(tn):
    per = HEADS * HEAD_W // tn
    nmap, tmap, mode = [], [], []
    n_seen = t_seen = 0
    n_total = per * _SLAB_TRANSPOSED.count(False)
    t_total = per * _SLAB_TRANSPOSED.count(True)
    for s in range(len(_SLAB_TRANSPOSED)):
        for _ in range(per):
            nmap.append(min(n_seen, n_total - 1))
            tmap.append(min(t_seen, t_total - 1))
            mode.append(s)
            if _SLAB_TRANSPOSED[s]:
                t_seen += 1
            else:
                n_seen += 1
    return (np.asarray(nmap, np.int32), np.asarray(tmap, np.int32), np.asarray(mode, np.int32))


def _inproj_kernel(nmap_ref, tmap_ref, mode_ref, x_ref, g_ref, w_ref, wfg_ref, cos_ref, sin_ref,
                   kn_ref, pt_ref, fg_ref, xn_ref, *, tn):
    j = pl.program_id(1)

    @pl.when(j == 0)
    def _():
        xn = _rms(x_ref[...], g_ref[...])
        xn_ref[...] = xn.astype(BF16)
        fg_ref[...] = _dot_f32ish(xn, wfg_ref[...])

    acc = _dot(xn_ref[...], w_ref[...].astype(BF16))
    reps = tn // LANES
    slab = mode_ref[j]

    def rope(a):
        cos = jnp.concatenate([cos_ref[...]] * reps, axis=1)
        sin = jnp.concatenate([sin_ref[...]] * reps, axis=1)
        lane = lax.broadcasted_iota(jnp.int32, a.shape, 1) % DIFF_QK_DIM
        rot = jnp.where(lane < DIFF_QK_DIM // 2,
                        pltpu.roll(a, tn - DIFF_QK_DIM // 2, 1),
                        pltpu.roll(a, DIFF_QK_DIM // 2, 1))
        return a * cos + rot * sin

    def store_t(a):
        at = a.T.astype(BF16)
        for c in range(pt_ref.shape[0]):
            pt_ref[c] = at[:, c * TC:(c + 1) * TC]

    for s in range(len(_SLAB_TRANSPOSED)):
        @pl.when(slab == s)
        def _(s=s):
            y = rope(acc) if _SLAB_ROPE[s] else acc
            if _SLAB_SCALE[s] != 1.0:
                y = y * _SLAB_SCALE[s]
            if _SLAB_TRANSPOSED[s]:
                store_t(y)
            else:
                kn_ref[...] = y.astype(BF16)


def _inproj(x, g, w_in, w_fg, cos, sin, *, tm, tn):
    t, d = x.shape
    slab = HEADS * HEAD_W
    nmap, tmap, mode = _inproj_tables(tn)
    n_t = _SLAB_TRANSPOSED.count(True)
    n_n = _SLAB_TRANSPOSED.count(False)
    grid_spec = pltpu.PrefetchScalarGridSpec(
        num_scalar_prefetch=3,
        grid=(t // tm, len(mode)),
        in_specs=[
            pl.BlockSpec((tm, d), lambda i, j, *_: (i, 0)),
            pl.BlockSpec((1, d), lambda i, j, *_: (0, 0)),
            pl.BlockSpec((d, tn), lambda i, j, *_: (0, j)),
            pl.BlockSpec((d, LANES), lambda i, j, *_: (0, 0)),
            pl.BlockSpec((tm, LANES), lambda i, j, *_: (i, 0)),
            pl.BlockSpec((tm, LANES), lambda i, j, *_: (i, 0)),
        ],
        out_specs=[
            pl.BlockSpec((tm, tn), lambda i, j, nmap, tmap, mode: (i, nmap[j])),
            pl.BlockSpec((tm // TC, tn, TC), lambda i, j, nmap, tmap, mode: (i, tmap[j], 0)),
            pl.BlockSpec((tm, LANES), lambda i, j, *_: (i, 0)),
        ],
        scratch_shapes=[pltpu.VMEM((tm, d), BF16)],
    )
    return pl.pallas_call(
        functools.partial(_inproj_kernel, tn=tn),
        grid_spec=grid_spec,
        out_shape=[
            jax.ShapeDtypeStruct((t, n_n * slab), BF16),
            jax.ShapeDtypeStruct((t // TC, n_t * slab, TC), BF16),
            jax.ShapeDtypeStruct((t, LANES), F32),
        ],
        compiler_params=_cparams("arbitrary", "arbitrary"),
        name="inproj",
    )(nmap, tmap, mode, x, g, w_in, w_fg, cos, sin)


def _forget_cumsum_kernel(fg_ref, b_ref, ka_ref, qat_ref, carry_ref):
    i = pl.program_id(0)

    @pl.when(i == 0)
    def _():
        carry_ref[...] = jnp.zeros_like(carry_ref)

    z = fg_ref[...] + b_ref[...]
    logf = jnp.minimum(z, 0.0) - jnp.log(1.0 + jnp.exp(-jnp.abs(z)))
    tm = logf.shape[0]
    row = lax.broadcasted_iota(jnp.int32, (tm, tm), 0)
    col = lax.broadcasted_iota(jnp.int32, (tm, tm), 1)
    tri = (row >= col).astype(BF16)
    c = carry_ref[0:1, :]
    for part in _split_bf16(logf, N_PARTS):
        c = c + _dot(tri, part)
    carry_ref[...] = jnp.broadcast_to(c[tm - 1:tm, :], carry_ref.shape)

    parts = [p.astype(F32) for p in _split_bf16(c * LOG2E, N_PARTS)]
    lane = lax.broadcasted_iota(jnp.int32, (tm, LANES), 1)
    ka = jnp.where((lane >= N_PARTS * HEADS) & (lane < N_PARTS * HEADS + N_PARTS), 1.0, 0.0)
    for p in range(N_PARTS):
        shifted = parts[p] if p == 0 else pltpu.roll(parts[p], p * HEADS, 1)
        ka = jnp.where((lane >= p * HEADS) & (lane < (p + 1) * HEADS), -shifted, ka)
    ka_ref[...] = ka.astype(BF16)

    parts_t = [p.T for p in parts]
    r = lax.broadcasted_iota(jnp.int32, (AUG_ROWS, tm), 0)
    for h in range(HEADS):
        blk = jnp.where((r < N_PARTS * HEADS) & (r % HEADS == h), 1.0, 0.0)
        for p in range(N_PARTS):
            blk = jnp.where(r == N_PARTS * HEADS + p, parts_t[p][h:h + 1, :], blk)
        blk = blk.astype(BF16)
        for ci in range(qat_ref.shape[0]):
            qat_ref[ci, h * AUG_ROWS:(h + 1) * AUG_ROWS, :] = blk[:, ci * TC:(ci + 1) * TC]


def _forget_cumsum(fg, b, *, tm):
    t = fg.shape[0]
    return pl.pallas_call(
        _forget_cumsum_kernel,
        grid=(t // tm,),
        in_specs=[pl.BlockSpec((tm, LANES), lambda i: (i, 0)),
                  pl.BlockSpec((1, LANES), lambda i: (0, 0))],
        out_specs=[pl.BlockSpec((tm, LANES), lambda i: (i, 0)),
                   pl.BlockSpec((tm // TC, HEADS * AUG_ROWS, TC), lambda i: (i, 0, 0))],
        out_shape=[jax.ShapeDtypeStruct((t, LANES), BF16),
                   jax.ShapeDtypeStruct((t // TC, HEADS * AUG_ROWS, TC), BF16)],
        scratch_shapes=[pltpu.VMEM((8, LANES), F32)],
        compiler_params=_cparams("arbitrary"),
        name="forget_cumsum",
    )(fg, b)


def _attn_kernel(*refs, ncomp, has_bias, tq):
    if has_bias:
        (qt_ref, k_ref, vt_ref, ka_ref, qat_ref, o_ref,
         qz_ref, m_ref, l_ref, acc_ref, sa_ref, sb_ref, kaug_ref) = refs
    else:
        (qt_ref, k_ref, vt_ref, lam_ref, subln_ref, o_ref,
         qz_ref, m_ref, l_ref, acc_ref, sa_ref, sb_ref) = refs
    i = pl.program_id(1)
    cols = ncomp * tq
    tk = TC

    qt = qt_ref[...]
    if ncomp == 2:
        row = lax.broadcasted_iota(jnp.int32, qt.shape, 0)
        zero = jnp.zeros_like(qt)
        qz_ref[:, 0:tq] = jnp.where(row < DIFF_QK_DIM, qt, zero)
        qz_ref[:, tq:cols] = jnp.where(row >= DIFF_QK_DIM, qt, zero)
        ksrc = k_ref
    else:
        qz_ref[0:HEAD_W, :] = qt
        qz_ref[HEAD_W:HEAD_W + AUG_ROWS, :] = qat_ref[...]
        qz_ref[HEAD_W + AUG_ROWS:, :] = jnp.zeros((qz_ref.shape[0] - HEAD_W - AUG_ROWS, cols), BF16)

        @pl.when(i == 0)
        def _():
            kaug_ref[:, 0:HEAD_W] = k_ref[...]
            kaug_ref[:, HEAD_W:] = ka_ref[...]
        ksrc = kaug_ref

    m_ref[...] = jnp.full_like(m_ref, NEG_BIG)
    l_ref[...] = jnp.zeros_like(l_ref)
    acc_ref[...] = jnp.zeros_like(acc_ref)

    def scores(j, s_ref, masked):
        start = pl.multiple_of(j * tk, tk)
        s = _dot(ksrc[pl.ds(start, tk), :], qz_ref[...])
        if masked:
            kpos = start + lax.broadcasted_iota(jnp.int32, (tk, cols), 0)
            qpos = i * tq + lax.broadcasted_iota(jnp.int32, (tk, cols), 1) % tq
            s = jnp.where(kpos <= qpos, s, NEG_BIG)
        s_ref[...] = s

    def absorb(j, s_ref):
        s = s_ref[...]
        m_prev = m_ref[...]
        m_new = jnp.maximum(m_prev, jnp.max(s, axis=0, keepdims=True))
        alpha = jnp.exp2(m_prev - m_new)
        p = jnp.exp2(s - m_new)
        l_ref[...] = alpha * l_ref[...] + jnp.sum(p, axis=0, keepdims=True)
        acc_ref[...] = alpha * acc_ref[...] + _dot(vt_ref[j], p.astype(BF16))
        m_ref[...] = m_new

    n_full = (i * tq) // tk
    n_pairs = jnp.maximum(n_full - 1, 0) // 2
    scores(0, sa_ref, True)

    def pair(g, carry):
        c = 2 * g
        scores(c + 1, sb_ref, False)
        absorb(c, sa_ref)
        scores(c + 2, sa_ref, False)
        absorb(c + 1, sb_ref)
        return carry

    lax.fori_loop(0, n_pairs, pair, 0)
    c0 = 2 * n_pairs
    left = n_full - c0 + 1

    @pl.when(left == 1)
    def _():
        absorb(c0, sa_ref)

    @pl.when(left == 2)
    def _():
        scores(c0 + 1, sb_ref, True)
        absorb(c0, sa_ref)
        absorb(c0 + 1, sb_ref)

    @pl.when(left == 3)
    def _():
        scores(c0 + 1, sb_ref, False)
        absorb(c0, sa_ref)
        scores(c0 + 2, sa_ref, True)
        absorb(c0 + 1, sb_ref)
        absorb(c0 + 2, sa_ref)

    o = acc_ref[...] * (1.0 / l_ref[...])
    if ncomp == 2:
        o = o[:, 0:tq] - lam_ref[0, 0] * o[:, tq:cols]
        ms = jnp.mean(o * o, axis=0, keepdims=True)
        o = o * lax.rsqrt(ms + NORM_EPS) * subln_ref[...] * (1.0 - LAM_INIT)
    o_ref[...] = o.T.astype(o_ref.dtype)


def _attention(kn, pt, q_blk, k_blk, v_blk, extra, *, ncomp, has_bias, tq):
    t = kn.shape[0]
    cols = ncomp * tq
    per = TC // tq
    in_specs = [
        pl.BlockSpec((None, HEAD_W, tq), lambda h, i: (i // per, q_blk + h, i % per)),
        pl.BlockSpec((t, HEAD_W), lambda h, i: (0, k_blk + h)),
        pl.BlockSpec((t // TC, HEAD_W, TC), lambda h, i: (0, v_blk + h, 0)),
    ]
    scratch = [pltpu.VMEM(((2 if has_bias else 1) * HEAD_W, cols), BF16),
               pltpu.VMEM((1, cols), F32),
               pltpu.VMEM((1, cols), F32),
               pltpu.VMEM((HEAD_W, cols), F32),
               pltpu.VMEM((TC, cols), F32),
               pltpu.VMEM((TC, cols), F32)]
    if has_bias:
        assert tq == TC
        in_specs += [pl.BlockSpec((t, LANES), lambda h, i: (0, 0)),
                     pl.BlockSpec((None, AUG_ROWS, tq), lambda h, i: (i, h, 0))]
        scratch += [pltpu.VMEM((t, 2 * HEAD_W), BF16)]
    else:
        in_specs += [pl.BlockSpec(memory_space=pltpu.SMEM),
                     pl.BlockSpec((HEAD_W, 1), lambda h, i: (0, 0))]
    return pl.pallas_call(
        functools.partial(_attn_kernel, ncomp=ncomp, has_bias=has_bias, tq=tq),
        grid=(HEADS, t // tq),
        in_specs=in_specs,
        out_specs=pl.BlockSpec((tq, HEAD_W), lambda h, i: (i, h)),
        out_shape=jax.ShapeDtypeStruct((t, HEADS * HEAD_W), BF16),
        scratch_shapes=scratch,
        compiler_params=_cparams("arbitrary", "arbitrary"),
        name="fox_attention" if has_bias else "diff_attention",
    )(pt, kn, pt, *extra)


ROW_SUB = 16


def _store_rows(ref, a):
    for c in range(ROW_SUB):
        ref[c] = a[:, c * LANES:(c + 1) * LANES]


def _load_rows(ref):
    return jnp.concatenate([ref[c] for c in range(ROW_SUB)], axis=1)


def _outproj_router_kernel(od_ref, of_ref, x_ref, wo_ref, g_ref, wr_ref, rb_ref,
                           h1_ref, hn_ref, hp_ref, idx_ref, rank_ref, gate_ref, cnt_ref, carry_ref):
    i = pl.program_id(0)

    @pl.when(i == 0)
    def _():
        carry_ref[...] = jnp.zeros_like(carry_ref)

    half = od_ref.shape[1]
    h1 = x_ref[...] + _dot(od_ref[...], wo_ref[0:half, :]) + _dot(of_ref[...], wo_ref[half:, :])
    h1_ref[...] = h1
    hn = _rms(h1, g_ref[...])
    hn_ref[...] = hn.astype(BF16)
    _store_rows(hp_ref, hn)
    scores = jax.nn.sigmoid(_dot_f32ish(hn, wr_ref[...]))
    tm = scores.shape[0]
    lane = lax.broadcasted_iota(jnp.int32, scores.shape, 1)
    sel = jnp.where(lane < N_EXPERTS, scores + rb_ref[...], -jnp.inf)
    hits, picks = [], []
    for _ in range(TOP_K):
        mx = jnp.max(sel, axis=1, keepdims=True)
        first = jnp.min(jnp.where(sel == mx, lane, LANES), axis=1, keepdims=True)
        hit = lane == first
        hits.append(hit)
        picks.append(first)
        sel = jnp.where(hit, -jnp.inf, sel)
    chosen = functools.reduce(jnp.logical_or, hits)
    chosen_f = jnp.where(chosen, 1.0, 0.0)

    r = lax.broadcasted_iota(jnp.int32, (tm, tm), 0)
    c = lax.broadcasted_iota(jnp.int32, (tm, tm), 1)
    before = carry_ref[0:1, :] + _dot((r > c).astype(BF16), chosen_f.astype(BF16))
    total = carry_ref[0:1, :] + jnp.sum(chosen_f, axis=0, keepdims=True)
    carry_ref[...] = jnp.broadcast_to(total, carry_ref.shape)
    cnt_ref[...] = jnp.broadcast_to(total, cnt_ref.shape).astype(jnp.int32)

    denom = jnp.sum(jnp.where(chosen, scores, 0.0), axis=1, keepdims=True)
    idx = jnp.zeros(scores.shape, jnp.int32)
    rank = jnp.zeros(scores.shape, jnp.int32)
    gate = jnp.zeros(scores.shape, F32)
    for k in range(TOP_K):
        score_k = jnp.sum(jnp.where(hits[k], scores, 0.0), axis=1, keepdims=True)
        rank_k = jnp.sum(jnp.where(hits[k], before, 0.0), axis=1, keepdims=True)
        idx = jnp.where(lane == k, picks[k], idx)
        rank = jnp.where(lane == k, rank_k.astype(jnp.int32), rank)
        gate = jnp.where(lane == k, score_k / denom * ROUTED_SCALE, gate)
    idx_ref[...] = idx
    rank_ref[...] = rank
    gate_ref[...] = gate


def _outproj_router(o_diff, o_fox, x, w_o, ffn_norm, w_router, router_bias, *, tm):
    t, d = x.shape
    half = o_diff.shape[1]
    row = lambda i: (i, 0)
    const = lambda i: (0, 0)
    return pl.pallas_call(
        _outproj_router_kernel,
        grid=(t // tm,),
        in_specs=[
            pl.BlockSpec((tm, half), row),
            pl.BlockSpec((tm, half), row),
            pl.BlockSpec((tm, d), row),
            pl.BlockSpec((d, d), const),
            pl.BlockSpec((1, d), const),
            pl.BlockSpec((d, LANES), const),
            pl.BlockSpec((1, LANES), const),
        ],
        out_specs=[pl.BlockSpec((tm, d), row), pl.BlockSpec((tm, d), row),
                   pl.BlockSpec((ROW_SUB, tm, LANES), lambda i: (0, i, 0)),
                   pl.BlockSpec((tm, LANES), row), pl.BlockSpec((tm, LANES), row),
                   pl.BlockSpec((tm, LANES), row), pl.BlockSpec((8, LANES), const)],
        out_shape=[jax.ShapeDtypeStruct((t, d), F32), jax.ShapeDtypeStruct((t, d), BF16),
                   jax.ShapeDtypeStruct((ROW_SUB, t, LANES), F32),
                   jax.ShapeDtypeStruct((t, LANES), jnp.int32), jax.ShapeDtypeStruct((t, LANES), jnp.int32),
                   jax.ShapeDtypeStruct((t, LANES), F32), jax.ShapeDtypeStruct((8, LANES), jnp.int32)],
        scratch_shapes=[pltpu.VMEM((8, LANES), F32)],
        compiler_params=_cparams("arbitrary"),
        name="outproj_router",
    )(o_diff, o_fox, x, w_o, ffn_norm, w_router, router_bias)


SCATTER_LAG = 1
Y_SLOTS = 3


def _moe_routed_kernel(te_ref, rows_ref, nv_ref, cur_ref, nxt_ref, prv_ref, xp_hbm, wg_ref, wu_ref, wd_ref,
                       y8_hbm, xbuf, ybuf, wgb, wub, wdb, sem_in, sem_out, *, n_tokens, tm):
    s = pl.program_id(0)
    n_valid = nv_ref[0]
    x_slot = s % 2

    def gather_row(table_ref, to_slot, r):
        tok = table_ref[0, r] & (n_tokens - 1)
        return pltpu.make_async_copy(xp_hbm.at[:, tok], xbuf.at[to_slot, :, r], sem_in.at[to_slot])

    def start_gather(table_ref, to_slot):
        for r in range(tm):
            gather_row(table_ref, to_slot, r).start()

    def wait_gather(of_slot):
        pltpu.make_async_copy(xp_hbm.at[:, pl.ds(0, tm)], xbuf.at[of_slot], sem_in.at[of_slot]).wait()

    def scatter_row(table_ref, from_slot, r):
        return pltpu.make_async_copy(ybuf.at[from_slot, :, r], y8_hbm.at[:, table_ref[0, r]],
                                     sem_out.at[from_slot])

    def for_rows(n_rows, full, one):
        @pl.when(n_rows == tm)
        def _():
            full()

        @pl.when((n_rows > 0) & (n_rows < tm))
        def _():
            def body(r, carry):
                one(r)
                return carry
            lax.fori_loop(0, n_rows, body, 0)

    def start_scatter(table_ref, tile):
        from_slot = tile % Y_SLOTS

        def full():
            for r in range(tm):
                scatter_row(table_ref, from_slot, r).start()
        for_rows(rows_ref[tile], full, lambda r: scatter_row(table_ref, from_slot, r).start())

    def wait_scatter(tile):
        of_slot = tile % Y_SLOTS

        def full():
            pltpu.make_async_copy(ybuf.at[of_slot], y8_hbm.at[:, pl.ds(0, tm)], sem_out.at[of_slot]).wait()
        for_rows(rows_ref[tile], full,
                 lambda r: pltpu.make_async_copy(ybuf.at[of_slot, :, 0], y8_hbm.at[:, 0],
                                                 sem_out.at[of_slot]).wait())

    @pl.when(s == 0)
    def _():
        start_gather(cur_ref, 0)

    @pl.when((s >= Y_SLOTS) & (s - Y_SLOTS < n_valid))
    def _():
        wait_scatter(s - Y_SLOTS)

    @pl.when((s < n_valid) & ((s == 0) | (te_ref[s] != te_ref[jnp.maximum(s - 1, 0)])))
    def _():
        wgb[...] = wg_ref[...].astype(BF16)
        wub[...] = wu_ref[...].astype(BF16)
        wdb[...] = wd_ref[...].astype(BF16)

    @pl.when(s < n_valid)
    def _():
        wait_gather(x_slot)
        start_gather(nxt_ref, 1 - x_slot)
        x = _load_rows(xbuf.at[x_slot]).astype(BF16)
        mid = (jax.nn.silu(_dot(x, wgb[...])) * _dot(x, wub[...])).astype(BF16)
        _store_rows(ybuf.at[s % Y_SLOTS], _dot(mid, wdb[...]))

    @pl.when(s == n_valid)
    def _():
        wait_gather(x_slot)

    @pl.when((s >= SCATTER_LAG) & (s - SCATTER_LAG < n_valid))
    def _():
        start_scatter(prv_ref, s - SCATTER_LAG)


def _moe_routed(hp, slot_table, tile_expert, tile_rows, n_valid, w_gate, w_up, w_down, *, n_tokens, tm):
    n_steps = slot_table.shape[0]
    n_e, d, f = w_gate.shape
    assert n_tokens & (n_tokens - 1) == 0 and ROW_SUB * LANES == d
    table_spec = lambda index: pl.BlockSpec((None, 1, tm), index, memory_space=pltpu.SMEM)
    grid_spec = pltpu.PrefetchScalarGridSpec(
        num_scalar_prefetch=3,
        grid=(n_steps,),
        in_specs=[
            table_spec(lambda s, *_: (s, 0, 0)),
            table_spec(lambda s, *_: (jnp.minimum(s + 1, n_steps - 1), 0, 0)),
            table_spec(lambda s, *_: (jnp.maximum(s - SCATTER_LAG, 0), 0, 0)),
            pl.BlockSpec(memory_space=pl.ANY),
            pl.BlockSpec((None, d, f), lambda s, te, *_: (te[s], 0, 0)),
            pl.BlockSpec((None, d, f), lambda s, te, *_: (te[s], 0, 0)),
            pl.BlockSpec((None, f, d), lambda s, te, *_: (te[s], 0, 0)),
        ],
        out_specs=pl.BlockSpec(memory_space=pl.ANY),
        scratch_shapes=[pltpu.VMEM((2, ROW_SUB, tm, LANES), F32),
                        pltpu.VMEM((Y_SLOTS, ROW_SUB, tm, LANES), F32),
                        pltpu.VMEM((d, f), BF16), pltpu.VMEM((d, f), BF16), pltpu.VMEM((f, d), BF16),
                        pltpu.SemaphoreType.DMA((2,)), pltpu.SemaphoreType.DMA((Y_SLOTS,))],
    )
    return pl.pallas_call(
        functools.partial(_moe_routed_kernel, n_tokens=n_tokens, tm=tm),
        grid_spec=grid_spec,
        out_shape=jax.ShapeDtypeStruct((ROW_SUB, TOP_K * n_tokens, LANES), F32),
        compiler_params=_cparams("arbitrary"),
        name="moe_routed",
    )(tile_expert, tile_rows, n_valid, slot_table, slot_table, slot_table, hp, w_gate, w_up, w_down)


def _routing_tables(idx, rank, counts, *, n_tokens, tm):
    n_e = counts.shape[0]
    n_steps = TOP_K * n_tokens // tm + n_e + Y_SLOTS
    tiles = (counts + tm - 1) // tm
    tile_end = jnp.cumsum(tiles)
    row0 = (tile_end - tiles) * tm
    pos = row0[idx] + rank
    slot = (jnp.arange(TOP_K, dtype=jnp.int32)[None, :] * n_tokens
            + jnp.arange(n_tokens, dtype=jnp.int32)[:, None])
    table = jnp.zeros((n_steps * tm,), jnp.int32).at[pos.reshape(-1)].set(slot.reshape(-1))
    n_valid = tile_end[-1].astype(jnp.int32)
    steps = jnp.arange(n_steps, dtype=jnp.int32)
    last = jnp.minimum(steps, n_valid - 1)
    tile_expert = jnp.sum((tile_end[None, :] <= last[:, None]).astype(jnp.int32), axis=1)
    rows_left = counts[tile_expert] - (steps - (tile_end - tiles)[tile_expert]) * tm
    tile_rows = jnp.where(steps < n_valid, jnp.clip(rows_left, 0, tm), 0).astype(jnp.int32)
    return table.reshape(n_steps, 1, tm), tile_expert, tile_rows, n_valid.reshape(1)


def _final_kernel(h1_ref, hn_ref, p_ref, gate_ref, *refs):
    y_refs = refs[:TOP_K]
    wsg_ref, wsu_ref, wsd_ref, wpg_ref, bpg_ref, wpp_ref, gple_ref, gfin_ref, o_ref = refs[TOP_K:]
    gates = gate_ref[...]
    y = gates[:, 0:1] * _load_rows(y_refs[0])
    for k in range(1, TOP_K):
        y = y + gates[:, k:k + 1] * _load_rows(y_refs[k])
    hn = hn_ref[...]
    mid = (jax.nn.silu(_dot(hn, wsg_ref[...])) * _dot(hn, wsu_ref[...])).astype(BF16)
    h2 = h1_ref[...] + y + _dot(mid, wsd_ref[...])
    gate = jax.nn.sigmoid(_dot(h2.astype(BF16), wpg_ref[...]) + bpg_ref[...])
    emb = _rms(_dot(p_ref[...].astype(BF16), wpp_ref[...]), gple_ref[...])
    o_ref[...] = _rms(h2 + gate * emb, gfin_ref[...])


def _final(h1, hn, p, gates, y8, ws_gate, ws_up, ws_down, w_ple_gate, b_ple_gate, w_ple_proj,
           ple_norm, final_norm, *, tm):
    t, d = h1.shape
    row = lambda i: (i, 0)
    const = lambda i: (0, 0)
    full = lambda a: pl.BlockSpec(a.shape, const, pipeline_mode=pl.Buffered(1))
    per = t // tm
    y_specs = [pl.BlockSpec((ROW_SUB, tm, LANES), functools.partial(lambda i, k: (0, k * per + i, 0), k=k))
               for k in range(TOP_K)]
    return pl.pallas_call(
        _final_kernel,
        grid=(t // tm,),
        in_specs=[pl.BlockSpec((tm, d), row), pl.BlockSpec((tm, d), row),
                  pl.BlockSpec((tm, p.shape[1]), row), pl.BlockSpec((tm, LANES), row)] + y_specs +
                 [full(ws_gate), full(ws_up), full(ws_down), full(w_ple_gate), full(b_ple_gate),
                  full(w_ple_proj), full(ple_norm), full(final_norm)],
        out_specs=pl.BlockSpec((tm, d), row),
        out_shape=jax.ShapeDtypeStruct((t, d), F32),
        compiler_params=_cparams("arbitrary"),
        name="final",
    )(h1, hn, p, gates, *([y8] * TOP_K), ws_gate, ws_up, ws_down, w_ple_gate, b_ple_gate, w_ple_proj,
      ple_norm, final_norm)


def _rope_tables(t):
    pos = jnp.arange(t, dtype=F32)
    inv_freq = ROPE_THETA ** (-jnp.arange(0, DIFF_QK_DIM, 2, dtype=F32) / DIFF_QK_DIM)
    ang = pos[:, None] * inv_freq[None, :]
    ang = jnp.concatenate([ang, ang, ang, ang], axis=-1)
    half_sign = jnp.where(jnp.arange(LANES) % DIFF_QK_DIM < DIFF_QK_DIM // 2, -1.0, 1.0).astype(F32)
    return jnp.cos(ang), jnp.sin(ang) * half_sign


def _pad_lanes(a):
    return jnp.pad(a, ((0, 0), (0, LANES - a.shape[1])))


def kernel(x, p, attn_norm, w_in, b_forget, lambda_q1, lambda_k1, lambda_q2, lambda_k2, subln, w_o,
           ffn_norm, w_router, router_bias, w_gate, w_up, w_down, ws_gate, ws_up, ws_down,
           w_ple_gate, b_ple_gate, w_ple_proj, ple_norm, final_norm):
    b, t, d = x.shape
    assert b == 1 and w_in.shape[0] == 1 and t % TC == 0
    x2 = x[0]
    slab = HEADS * HEAD_W
    ncol = 6 * slab

    cos, sin = _rope_tables(t)
    w_fg = _pad_lanes(w_in[0][:, ncol:])
    kn, pt, fg = _inproj(x2, attn_norm, w_in[0], w_fg, cos, sin, tm=min(t, 1024), tn=512)

    ka, qat = _forget_cumsum(fg, _pad_lanes(b_forget), tm=min(t, 1024))

    lam = (jnp.exp(jnp.sum(lambda_q1[0] * lambda_k1[0])) - jnp.exp(jnp.sum(lambda_q2[0] * lambda_k2[0]))
           + LAM_INIT).reshape(1, 1).astype(F32)
    o_diff = _attention(kn, pt, 0, 0, HEADS, (lam, subln.reshape(HEAD_W, 1)),
                        ncomp=2, has_bias=False, tq=TC // 2)
    o_fox = _attention(kn, pt, 2 * HEADS, HEADS, 3 * HEADS, (ka, qat),
                       ncomp=1, has_bias=True, tq=TC)

    h1, hn, hp, idx, rank, gates, counts = _outproj_router(
        o_diff, o_fox, x2, w_o[0].astype(BF16), ffn_norm,
        _pad_lanes(w_router[0]), _pad_lanes(router_bias), tm=min(t, 512))

    slot_table, tile_expert, tile_rows, n_valid = _routing_tables(
        idx[:, :TOP_K], rank[:, :TOP_K], counts[0, :N_EXPERTS], n_tokens=t, tm=ROW_TILE)
    y8 = _moe_routed(hp, slot_table, tile_expert, tile_rows, n_valid, w_gate[0], w_up[0], w_down[0],
                     n_tokens=t, tm=ROW_TILE)

    out = _final(h1, hn, p[0, 0], gates, y8, ws_gate[0].astype(BF16), ws_up[0].astype(BF16),
                 ws_down[0].astype(BF16), w_ple_gate[0].astype(BF16), b_ple_gate,
                 w_ple_proj[0].astype(BF16), ple_norm, final_norm.reshape(1, d), tm=min(t, 128))
    return out.reshape(b, t, d)
```
